```python
import math
import jax, jax.numpy as jnp
from jax import lax
import numpy as np

D_MODEL = 4096
BATCH = 1
SEQ = 8192
DEPTH = 2

BRANCH_WIDTH = D_MODEL // 2
HEAD_DIM = 128
SB_HEADS = BRANCH_WIDTH // HEAD_DIM
RET_DK = 128
RET_DV = 256
RET_HEADS = BRANCH_WIDTH // RET_DV
GM_GROUP_DIM = 128
GM_GROUPS = BRANCH_WIDTH // GM_GROUP_DIM
CHUNK = 128
N_BRANCH = 3
D_FF = 2 * D_MODEL
PLE_DIM = 256
ROPE_BASE = 10000.0
EPS = 1e-6

SB_W = SB_HEADS * HEAD_DIM
RET_QK_W = RET_HEADS * RET_DK
RET_V_W = RET_HEADS * RET_DV
GM_W = GM_GROUPS * GM_GROUP_DIM
SPLIT_SIZES = (SB_W, SB_W, SB_W, RET_QK_W, RET_QK_W, RET_V_W, RET_V_W, GM_W, GM_W, N_BRANCH * D_MODEL)
N_IN = 3 * SB_W + 2 * RET_QK_W + 2 * RET_V_W + 2 * GM_W + N_BRANCH * D_MODEL

kernel_name = "hybrid_stickbreak_retention_gmlp_macaron"


def rms_norm(x, g):
    xf = x.astype(jnp.float32)
    y = xf * lax.rsqrt(jnp.mean(xf * xf, axis=-1, keepdims=True) + EPS)
    return (y * g.astype(jnp.float32)).astype(x.dtype)


def swiglu_ffn(x, w_gu, w_down):
    g, u = jnp.split(x @ w_gu, 2, axis=-1)
    return (jax.nn.silu(g) * u) @ w_down


def rope(x):
    s, d = x.shape[1], x.shape[-1]
    half = d // 2
    freq = ROPE_BASE ** (-jnp.arange(half, dtype=jnp.float32) / half)
    ang = jnp.arange(s, dtype=jnp.float32)[:, None] * freq[None, :]
    cos = jnp.cos(ang)[None, :, None, :]
    sin = jnp.sin(ang)[None, :, None, :]
    xf = x.astype(jnp.float32)
    x1, x2 = xf[..., :half], xf[..., half:]
    return jnp.concatenate([x1 * cos - x2 * sin, x1 * sin + x2 * cos], axis=-1).astype(x.dtype)


def stick_breaking_attention(q, k, v):
    b, s, h, d = q.shape
    nb = s // CHUNK
    scale = d ** -0.5
    qb = q.reshape(b, nb, CHUNK, h, d).transpose(1, 0, 3, 2, 4)
    kt = k.transpose(0, 2, 1, 3)
    vt = v.transpose(0, 2, 1, 3)
    key_pos = jnp.arange(s)

    def block(args):
        qi, bi = args
        z = jnp.einsum('bhqd,bhkd->bhqk', qi, kt).astype(jnp.float32) * scale
        q_pos = bi * CHUNK + jnp.arange(CHUNK)
        mask = key_pos[None, :] < q_pos[:, None]
        log_beta = jax.nn.log_sigmoid(z)
        log_1mb = jnp.where(mask, log_beta - z, 0.0)
        tail = lax.cumsum(log_1mb, axis=3, reverse=True) - log_1mb
        w = jnp.where(mask, jnp.exp(log_beta + tail), 0.0)
        return jnp.einsum('bhqk,bhkd->bhqd', w.astype(vt.dtype), vt)

    out = lax.map(block, (qb, jnp.arange(nb)))
    return out.transpose(1, 0, 3, 2, 4).reshape(b, s, h * d)


def chunkwise_retention(q, k, v):
    b, s, h, dk = q.shape
    dv = v.shape[-1]
    nc = s // CHUNK
    log_gamma = jnp.log(1.0 - 2.0 ** (-5.0 - jnp.arange(h, dtype=jnp.float32)))
    idx = jnp.arange(CHUNK, dtype=jnp.float32)
    diff = idx[:, None] - idx[None, :]
    decay_in = jnp.where(diff >= 0, jnp.exp(log_gamma[:, None, None] * jnp.maximum(diff, 0.0)), 0.0)
    q_decay = jnp.exp(log_gamma[:, None] * (idx + 1.0))[None, :, :, None]
    k_decay = jnp.exp(log_gamma[:, None] * (CHUNK - 1.0 - idx))[None, :, :, None]
    chunk_decay = jnp.exp(log_gamma * CHUNK)[None, :, None, None]

    def to_chunks(t):
        return t.astype(jnp.float32).reshape(b, nc, CHUNK, h, t.shape[-1]).transpose(1, 0, 3, 2, 4)

    qc, kc, vc = to_chunks(q), to_chunks(k * (dk ** -0.5)), to_chunks(v)

    def step(state, inp):
        qi, ki, vi = inp
        inner = jnp.einsum('bhnd,bhmd->bhnm', qi, ki) * decay_in[None]
        o = jnp.einsum('bhnm,bhmv->bhnv', inner, vi) + jnp.einsum('bhnd,bhdv->bhnv', qi, state) * q_decay
        state = state * chunk_decay + jnp.einsum('bhmd,bhmv->bhdv', ki * k_decay, vi)
        return state, o

    state0 = jnp.zeros((b, h, dk, dv), jnp.float32)
    _, out = lax.scan(step, state0, (qc, kc, vc))
    return out.transpose(1, 0, 3, 2, 4).reshape(b, s, h, dv)


def head_group_norm(x, g):
    mu = jnp.mean(x, axis=-1, keepdims=True)
    xc = x - mu
    y = xc * lax.rsqrt(jnp.mean(xc * xc, axis=-1, keepdims=True) + EPS)
    return y * g.astype(jnp.float32).reshape(x.shape[2], x.shape[3])


def chunk_spatial_gating(u, v, ln_g, ln_b, w_s, b_s):
    b, s, g, cg = v.shape
    nc = s // CHUNK
    vf = v.astype(jnp.float32)
    mu = jnp.mean(vf, axis=-1, keepdims=True)
    vcent = vf - mu
    vn = vcent * lax.rsqrt(jnp.mean(vcent * vcent, axis=-1, keepdims=True) + EPS)
    vn = (vn * ln_g.astype(jnp.float32).reshape(g, cg) + ln_b.astype(jnp.float32).reshape(g, cg)).astype(v.dtype)
    causal = jnp.tril(jnp.ones((CHUNK, CHUNK), dtype=bool))
    w = jnp.where(causal[None], w_s, 0.0).astype(v.dtype)
    vc = vn.reshape(b, nc, CHUNK, g, cg)
    mixed = jnp.einsum('gts,bcsgd->bctgd', w, vc) + b_s.T[None, None, :, :, None]
    return (u * mixed.reshape(b, s, g, cg)).reshape(b, s, g * cg)


def token_mixing(n, w_in, sb_q_norm, sb_k_norm, ret_q_norm, ret_k_norm, ret_gn_g,
                 gm_ln_g, gm_ln_b, gm_w_s, gm_b_s, w_branch, w_out):
    b, s, _ = n.shape
    z = n @ w_in
    (sb_q, sb_k, sb_v, r_q, r_k, r_v, r_g, gm_u, gm_v, gates) = jnp.split(
        z, np.cumsum(SPLIT_SIZES)[:-1].tolist(), axis=-1)

    qa = rms_norm(sb_q.reshape(b, s, SB_HEADS, HEAD_DIM), sb_q_norm)
    ka = rms_norm(sb_k.reshape(b, s, SB_HEADS, HEAD_DIM), sb_k_norm)
    va = sb_v.reshape(b, s, SB_HEADS, HEAD_DIM)
    y_a = stick_breaking_attention(qa, ka, va)

    qb = rope(rms_norm(r_q.reshape(b, s, RET_HEADS, RET_DK), ret_q_norm))
    kb = rope(rms_norm(r_k.reshape(b, s, RET_HEADS, RET_DK), ret_k_norm))
    vb = r_v.reshape(b, s, RET_HEADS, RET_DV)
    ret = head_group_norm(chunkwise_retention(qb, kb, vb), ret_gn_g).reshape(b, s, RET_V_W)
    y_b = (jax.nn.silu(r_g.astype(jnp.float32)) * ret).astype(n.dtype)

    u = jax.nn.gelu(gm_u).reshape(b, s, GM_GROUPS, GM_GROUP_DIM)
    v = jax.nn.gelu(gm_v).reshape(b, s, GM_GROUPS, GM_GROUP_DIM)
    y_c = chunk_spatial_gating(u, v, gm_ln_g, gm_ln_b, gm_w_s, gm_b_s)

    ys = jnp.stack([y_a, y_b, y_c], axis=2)
    proj = jnp.einsum('bsnw,nwd->bsnd', ys, w_branch)
    g = jax.nn.sigmoid(gates.reshape(b, s, N_BRANCH, D_MODEL))
    merged = jnp.sum(g * proj, axis=2)
    return merged @ w_out


def setup_inputs(seed: int = 0) -> dict:
    key = jax.random.key(seed)
    ks = jax.random.split(key, 32)
    f32 = jnp.float32
    L, D = DEPTH, D_MODEL

    def nrm(k, shape, scale):
        return jax.random.normal(k, shape, f32) * scale

    def gain(k, shape):
        return 1.0 + 0.02 * jax.random.normal(k, shape, f32)

    return {
        "x": nrm(ks[0], (BATCH, SEQ, D), 1.0),
        "p": nrm(ks[1], (DEPTH, BATCH, SEQ, PLE_DIM), 1.0),
        "ffn1_norm": gain(ks[2], (L, D)),
        "ffn1_w_gu": nrm(ks[3], (L, D, 2 * D_FF), D ** -0.5),
        "ffn1_w_down": nrm(ks[4], (L, D_FF, D), D_FF ** -0.5),
        "mix_norm": gain(ks[5], (L, D)),
        "w_in": nrm(ks[6], (L, D, N_IN), D ** -0.5),
        "sb_q_norm": gain(ks[7], (L, HEAD_DIM)),
        "sb_k_norm": gain(ks[8], (L, HEAD_DIM)),
        "ret_q_norm": gain(ks[9], (L, RET_DK)),
        "ret_k_norm": gain(ks[10], (L, RET_DK)),
        "ret_gn_g": gain(ks[11], (L, RET_V_W)),
        "gm_ln_g": gain(ks[12], (L, GM_W)),
        "gm_ln_b": nrm(ks[13], (L, GM_W), 0.02),
        "gm_w_s": nrm(ks[14], (L, GM_GROUPS, CHUNK, CHUNK), CHUNK ** -0.5),
        "gm_b_s": 1.0 + nrm(ks[15], (L, GM_GROUPS, CHUNK), 0.1),
        "w_branch": nrm(ks[16], (L, N_BRANCH, BRANCH_WIDTH, D), BRANCH_WIDTH ** -0.5),
        "w_out": nrm(ks[17], (L, D, D), D ** -0.5),
        "ffn2_norm": gain(ks[18], (L, D)),
        "ffn2_w_gu": nrm(ks[19], (L, D, 2 * D_FF), D ** -0.5),
        "ffn2_w_down": nrm(ks[20], (L, D_FF, D), D_FF ** -0.5),
        "ple_norm": gain(ks[21], (L, D)),
        "ple_w_gate": nrm(ks[22], (L, D, D), D ** -0.5),
        "ple_w_proj": nrm(ks[23], (L, PLE_DIM, D), PLE_DIM ** -0.5),
        "ple_post_norm": gain(ks[24], (L, D)),
    }


def reference(x, p, ffn1_norm, ffn1_w_gu, ffn1_w_down, mix_norm, w_in, sb_q_norm, sb_k_norm,
              ret_q_norm, ret_k_norm, ret_gn_g, gm_ln_g, gm_ln_b, gm_w_s, gm_b_s, w_branch, w_out,
              ffn2_norm, ffn2_w_gu, ffn2_w_down, ple_norm, ple_w_gate, ple_w_proj, ple_post_norm):
    h = x
    for i in range(DEPTH):
        h = h + 0.5 * swiglu_ffn(rms_norm(h, ffn1_norm[i]), ffn1_w_gu[i], ffn1_w_down[i])
        n = rms_norm(h, mix_norm[i])
        h = h + token_mixing(n, w_in[i], sb_q_norm[i], sb_k_norm[i], ret_q_norm[i], ret_k_norm[i],
                             ret_gn_g[i], gm_ln_g[i], gm_ln_b[i], gm_w_s[i], gm_b_s[i],
                             w_branch[i], w_out[i])
        h = h + 0.5 * swiglu_ffn(rms_norm(h, ffn2_norm[i]), ffn2_w_gu[i], ffn2_w_down[i])
        gate = jax.nn.sigmoid(rms_norm(h, ple_norm[i]) @ ple_w_gate[i])
        emb = rms_norm(p[i] @ ple_w_proj[i], ple_post_norm[i])
        h = h + gate * emb
    return h
```

```python
import functools

import jax
import jax.numpy as jnp
from jax import lax
from jax.experimental import pallas as pl
from jax.experimental.pallas import tpu as pltpu

F32 = jnp.float32
BF16 = jnp.bfloat16

HEAD_DIM = 128
RET_DK = 128
RET_DV = 256
GM_GROUP_DIM = 128
CHUNK = 128
N_BRANCH = 3
ROPE_BASE = 10000.0
EPS = 1e-6

VMEM_BYTES_V7X = 64 * 1024 * 1024
VMEM_CAP = VMEM_BYTES_V7X - 6 * 1024 * 1024

TILES = dict(
    mm_bm=1024,
    mm_bn=512,
    gu_bn=256,
    merge_bm=512,
    norm_bm=256,
    sb_bq=256,
    ret_rows=512,
    gm_rows=256,
    ple_bm=256,
)


def _tile(name, dim):
    t = min(TILES[name], dim)
    assert dim % t == 0, (name, dim, t)
    return t


def _params(sem, vmem_est):
    limit = int(min(max(vmem_est * 1.15 + (4 << 20), 32 << 20), VMEM_CAP))
    return pltpu.CompilerParams(dimension_semantics=sem, vmem_limit_bytes=limit)


def _nbytes(shape, dtype):
    n = 1
    for s in shape:
        n *= s
    return n * jnp.dtype(dtype).itemsize


def _sigmoid(x):
    return 1.0 / (1.0 + jnp.exp(-x))


def _silu(x):
    return x * _sigmoid(x)


def _gelu_tanh(x):
    c = 0.7978845608028654
    return 0.5 * x * (1.0 + jnp.tanh(c * (x + 0.044715 * (x * x * x))))


def _rmsnorm_kernel(x_ref, g_ref, o_ref):
    x = x_ref[...]
    inv = lax.rsqrt(jnp.mean(x * x, axis=-1, keepdims=True) + EPS)
    o_ref[...] = (x * inv * g_ref[...]).astype(o_ref.dtype)


def _rmsnorm(h, gains, layer):
    s, d = h.shape
    bm = _tile("norm_bm", s)
    est = 2 * (_nbytes((bm, d), F32) + _nbytes((bm, d), BF16)) + 2 * _nbytes((bm, d), F32)
    return pl.pallas_call(
        _rmsnorm_kernel,
        grid=(s // bm,),
        in_specs=[
            pl.BlockSpec((bm, d), lambda i: (i, 0)),
            pl.BlockSpec((None, 1, d), lambda i: (layer, 0, 0)),
        ],
        out_specs=pl.BlockSpec((bm, d), lambda i: (i, 0)),
        out_shape=jax.ShapeDtypeStruct((s, d), BF16),
        compiler_params=_params(("arbitrary",), est),
        name="rmsnorm",
    )(h, gains.reshape(gains.shape[0], 1, d))


def _mm_kernel(*refs, n_x, x_index, n_e, epilogue):
    n_w = len(x_index)
    x_refs = refs[:n_x]
    w_refs = refs[n_x:n_x + n_w]
    e_refs = refs[n_x + n_w:n_x + n_w + n_e]
    o_ref = refs[n_x + n_w + n_e]
    wb_refs = refs[n_x + n_w + n_e + 1:]

    @pl.when(pl.program_id(1) == 0)
    def _():
        for w_ref, wb_ref in zip(w_refs, wb_refs):
            wb_ref[...] = w_ref[...].astype(BF16)

    accs = [
        jnp.dot(x_refs[xi][...], wb_ref[...], preferred_element_type=F32)
        for xi, wb_ref in zip(x_index, wb_refs)
    ]
    o_ref[...] = epilogue(accs, [e[...] for e in e_refs]).astype(o_ref.dtype)


def _matmul(xs, ws, extras, epilogue, n_out, out_dtype, bm, bn, name):
    s = xs[0][0].shape[0]
    assert s % bm == 0 and n_out % bn == 0
    in_specs, args = [], []
    est = 0
    for arr, k, cb in xs:
        in_specs.append(pl.BlockSpec((bm, k), lambda j, i, cb=cb: (i, cb)))
        args.append(arr)
        est += 2 * _nbytes((bm, k), arr.dtype)
    scratch = []
    for arr, lead, xi, rb, co in ws:
        k = xs[xi][1]
        nlead = len(lead)
        in_specs.append(pl.BlockSpec(
            (None,) * nlead + (k, bn),
            lambda j, i, lead=lead, rb=rb, co=co: tuple(lead) + (rb, j + co)))
        args.append(arr)
        scratch.append(pltpu.VMEM((k, bn), BF16))
        est += 2 * _nbytes((k, bn), arr.dtype) + _nbytes((k, bn), BF16)
        est += _nbytes((bm, bn), F32)
    for arr, co in extras:
        in_specs.append(pl.BlockSpec((bm, bn), lambda j, i, co=co: (i, j + co)))
        args.append(arr)
        est += 2 * _nbytes((bm, bn), arr.dtype)
    est += 2 * _nbytes((bm, bn), out_dtype) + 2 * _nbytes((bm, bn), F32)
    kern = functools.partial(
        _mm_kernel, n_x=len(xs), x_index=tuple(w[2] for w in ws), n_e=len(extras),
        epilogue=epilogue)
    return pl.pallas_call(
        kern,
        grid=(n_out // bn, s // bm),
        in_specs=in_specs,
        out_specs=pl.BlockSpec((bm, bn), lambda j, i: (i, j)),
        out_shape=jax.ShapeDtypeStruct((s, n_out), out_dtype),
        scratch_shapes=scratch,
        compiler_params=_params(("arbitrary", "arbitrary"), est),
        name=name,
    )(*args)


def _ep_swiglu(accs, extras):
    g, u = accs
    return _silu(g) * u


def _ep_identity(accs, extras):
    return accs[0]


def _ep_residual(scale):
    def ep(accs, extras):
        return extras[0] + scale * accs[0]
    return ep


def _ep_merge(accs, extras):
    out = _sigmoid(extras[0].astype(F32)) * accs[0]
    for b in range(1, N_BRANCH):
        out = out + _sigmoid(extras[b].astype(F32)) * accs[b]
    return out


def _ep_ple(accs, extras):
    h, emb = extras
    return h + _sigmoid(accs[0]) * emb.astype(F32)


def _swiglu_ffn_half(h, n, w_gu, w_down, layer):
    s, d = n.shape
    d_ff = w_down.shape[1]
    bm = _tile("mm_bm", s)
    bn_gu = _tile("gu_bn", d_ff)
    a = _matmul(
        xs=[(n, d, 0)],
        ws=[(w_gu, (layer,), 0, 0, 0), (w_gu, (layer,), 0, 0, d_ff // bn_gu)],
        extras=[], epilogue=_ep_swiglu, n_out=d_ff, out_dtype=BF16,
        bm=bm, bn=bn_gu, name="ffn_gate_up")
    bn = _tile("mm_bn", d)
    kc = min(d, d_ff)
    assert d_ff % kc == 0
    for kb in range(d_ff // kc):
        h = _matmul(
            xs=[(a, kc, kb)],
            ws=[(w_down, (layer,), 0, kb, 0)],
            extras=[(h, 0)], epilogue=_ep_residual(0.5), n_out=d, out_dtype=F32,
            bm=bm, bn=bn, name="ffn_down")
    return h


def _sb_kernel(q_ref, k_ref, v_ref, gq_ref, gk_ref, u_ref, o_ref, kn_ref, *, bq, scale, norm_rows):
    qi = pl.program_id(1)
    s_len = k_ref.shape[0]

    @pl.when(qi == 0)
    def _():
        def norm_chunk(c, carry):
            r0 = pl.multiple_of(c * norm_rows, norm_rows)
            kf = k_ref[pl.ds(r0, norm_rows), :].astype(F32)
            inv = lax.rsqrt(jnp.mean(kf * kf, axis=-1, keepdims=True) + EPS)
            kn_ref[pl.ds(r0, norm_rows), :] = (kf * inv * gk_ref[...]).astype(BF16)
            return carry
        lax.fori_loop(0, s_len // norm_rows, norm_chunk, 0)

    qf = q_ref[...].astype(F32)
    inv_q = lax.rsqrt(jnp.mean(qf * qf, axis=-1, keepdims=True) + EPS)
    qn = (qf * inv_q * (gq_ref[...] * scale)).astype(BF16)
    upper = u_ref[...]

    row = lax.broadcasted_iota(jnp.int32, (bq, bq), 0)
    col = lax.broadcasted_iota(jnp.int32, (bq, bq), 1)
    causal = col < row

    def tile(j, carry, acc, masked):
        r0 = pl.multiple_of(j * bq, bq)
        kj = kn_ref[pl.ds(r0, bq), :]
        vj = v_ref[pl.ds(r0, bq), :]
        z = lax.dot_general(qn, kj, (((1,), (1,)), ((), ())), preferred_element_type=F32)
        log_beta = jnp.minimum(z, 0.0) - jnp.log(1.0 + jnp.exp(-jnp.abs(z)))
        log_1mb = log_beta - z
        if masked:
            log_1mb = jnp.where(causal, log_1mb, 0.0)
        cum = jnp.dot(log_1mb.astype(BF16), upper, preferred_element_type=F32)
        w = jnp.exp(log_beta + cum + carry)
        if masked:
            w = jnp.where(causal, w, 0.0)
        acc = acc + jnp.dot(w.astype(BF16), vj, preferred_element_type=F32)
        carry = carry + jnp.sum(log_1mb, axis=-1, keepdims=True)
        return carry, acc

    carry = jnp.zeros((bq, 1), F32)
    acc = jnp.zeros((bq, o_ref.shape[1]), F32)
    carry, acc = tile(qi, carry, acc, True)

    def body(it, c):
        return tile(qi - 1 - it, c[0], c[1], False)

    carry, acc = lax.fori_loop(0, qi, body, (carry, acc))
    o_ref[...] = acc.astype(o_ref.dtype)


def _stick_breaking(z, gq, gk, layer, sb_w):
    s = z.shape[0]
    heads = sb_w // HEAD_DIM
    bq = _tile("sb_bq", s)
    upper = (jnp.arange(bq)[:, None] > jnp.arange(bq)[None, :]).astype(BF16)
    kern = functools.partial(_sb_kernel, bq=bq, scale=HEAD_DIM ** -0.5, norm_rows=min(512, s))
    est = (4 * _nbytes((s, HEAD_DIM), BF16) + _nbytes((s, HEAD_DIM), BF16)
           + 10 * _nbytes((bq, bq), F32) + 2 * _nbytes((bq, bq), BF16))
    return pl.pallas_call(
        kern,
        grid=(heads, s // bq),
        in_specs=[
            pl.BlockSpec((bq, HEAD_DIM), lambda h, i: (i, h)),
            pl.BlockSpec((s, HEAD_DIM), lambda h, i: (0, heads + h)),
            pl.BlockSpec((s, HEAD_DIM), lambda h, i: (0, 2 * heads + h)),
            pl.BlockSpec((None, 1, HEAD_DIM), lambda h, i: (layer, 0, 0)),
            pl.BlockSpec((None, 1, HEAD_DIM), lambda h, i: (layer, 0, 0)),
            pl.BlockSpec((bq, bq), lambda h, i: (0, 0)),
        ],
        out_specs=pl.BlockSpec((bq, HEAD_DIM), lambda h, i: (i, h)),
        out_shape=jax.ShapeDtypeStruct((s, sb_w), BF16),
        scratch_shapes=[pltpu.VMEM((s, HEAD_DIM), BF16)],
        compiler_params=_params(("arbitrary", "arbitrary"), est),
        name="stick_breaking",
    )(z, z, z, gq.reshape(-1, 1, HEAD_DIM), gk.reshape(-1, 1, HEAD_DIM), upper)


def _ret_kernel(q_ref, k_ref, v_ref, g_ref, cos_ref, sin_ref, din_ref, qd_ref, kd_ref, cd_ref,
                gq_ref, gk_ref, gn_ref, o_ref, state_ref, *, n_chunks, k_scale):
    @pl.when(pl.program_id(1) == 0)
    def _():
        state_ref[...] = jnp.zeros_like(state_ref)

    def normed_rope(x_ref, gain_ref, r0):
        x = x_ref[r0:r0 + CHUNK, :].astype(F32)
        x = x * lax.rsqrt(jnp.mean(x * x, axis=-1, keepdims=True) + EPS) * gain_ref[...]
        swapped = pltpu.roll(x, RET_DK // 2, 1)
        return x * cos_ref[r0:r0 + CHUNK, :] + swapped * sin_ref[r0:r0 + CHUNK, :]

    for c in range(n_chunks):
        r0 = c * CHUNK
        q = normed_rope(q_ref, gq_ref, r0)
        k = normed_rope(k_ref, gk_ref, r0) * k_scale
        v = v_ref[r0:r0 + CHUNK, :]
        qb = q.astype(BF16)
        inner = lax.dot_general(qb, k.astype(BF16), (((1,), (1,)), ((), ())),
                                preferred_element_type=F32) * din_ref[...]
        state = state_ref[...]
        o = jnp.dot(inner.astype(BF16), v, preferred_element_type=F32)
        o = o + jnp.dot(qb, state.astype(BF16), preferred_element_type=F32) * qd_ref[...]
        kd = (k * kd_ref[...]).astype(BF16)
        state_ref[...] = state * cd_ref[...] + lax.dot_general(
            kd, v, (((0,), (0,)), ((), ())), preferred_element_type=F32)
        mu = jnp.mean(o, axis=-1, keepdims=True)
        oc = o - mu
        y = oc * lax.rsqrt(jnp.mean(oc * oc, axis=-1, keepdims=True) + EPS) * gn_ref[...]
        gate = g_ref[r0:r0 + CHUNK, :].astype(F32)
        o_ref[r0:r0 + CHUNK, :] = (_silu(gate) * y).astype(o_ref.dtype)


def _retention_tables(heads, s):
    hh = jnp.arange(heads, dtype=F32)
    log_gamma = jnp.log(1.0 - 2.0 ** (-5.0 - hh))
    idx = jnp.arange(CHUNK, dtype=F32)
    diff = idx[:, None] - idx[None, :]
    decay_in = jnp.where(diff >= 0, jnp.exp(log_gamma[:, None, None] * jnp.maximum(diff, 0.0)), 0.0)
    q_decay = jnp.exp(log_gamma[:, None] * (idx + 1.0))
    k_decay = jnp.exp(log_gamma[:, None] * (CHUNK - 1.0 - idx))
    chunk_decay = jnp.exp(log_gamma * CHUNK)
    qd = jnp.broadcast_to(q_decay[:, :, None], (heads, CHUNK, RET_DV))
    kd = jnp.broadcast_to(k_decay[:, :, None], (heads, CHUNK, RET_DK))
    cd = jnp.broadcast_to(chunk_decay[:, None, None], (heads, 1, RET_DV))
    half = RET_DK // 2
    freq = ROPE_BASE ** (-jnp.arange(half, dtype=F32) / half)
    ang = jnp.arange(s, dtype=F32)[:, None] * freq[None, :]
    cos = jnp.concatenate([jnp.cos(ang), jnp.cos(ang)], axis=-1)
    sin = jnp.concatenate([-jnp.sin(ang), jnp.sin(ang)], axis=-1)
    return decay_in, qd, kd, cd, cos, sin


def _retention(z, gq, gk, gn, layer, off_q, ret_qk_w, ret_v_w):
    s = z.shape[0]
    heads = ret_v_w // RET_DV
    rows = _tile("ret_rows", s)
    assert rows % CHUNK == 0
    decay_in, qd, kd, cd, cos, sin = _retention_tables(heads, s)
    off_k = off_q + ret_qk_w
    off_v = off_k + ret_qk_w
    off_g = off_v + ret_v_w
    assert off_q % RET_DK == 0 and off_v % RET_DV == 0 and off_g % RET_DV == 0
    kern = functools.partial(_ret_kernel, n_chunks=rows // CHUNK, k_scale=RET_DK ** -0.5)
    est = 2 * (2 * _nbytes((rows, RET_DK), BF16) + 3 * _nbytes((rows, RET_DV), BF16)
               + 2 * _nbytes((rows, RET_DK), F32) + 4 * _nbytes((CHUNK, RET_DV), F32))
    return pl.pallas_call(
        kern,
        grid=(heads, s // rows),
        in_specs=[
            pl.BlockSpec((rows, RET_DK), lambda h, i: (i, off_q // RET_DK + h)),
            pl.BlockSpec((rows, RET_DK), lambda h, i: (i, off_k // RET_DK + h)),
            pl.BlockSpec((rows, RET_DV), lambda h, i: (i, off_v // RET_DV + h)),
            pl.BlockSpec((rows, RET_DV), lambda h, i: (i, off_g // RET_DV + h)),
            pl.BlockSpec((rows, RET_DK), lambda h, i: (i, 0)),
            pl.BlockSpec((rows, RET_DK), lambda h, i: (i, 0)),
            pl.BlockSpec((None, CHUNK, CHUNK), lambda h, i: (h, 0, 0)),
            pl.BlockSpec((None, CHUNK, RET_DV), lambda h, i: (h, 0, 0)),
            pl.BlockSpec((None, CHUNK, RET_DK), lambda h, i: (h, 0, 0)),
            pl.BlockSpec((None, 1, RET_DV), lambda h, i: (h, 0, 0)),
            pl.BlockSpec((None, 1, RET_DK), lambda h, i: (layer, 0, 0)),
            pl.BlockSpec((None, 1, RET_DK), lambda h, i: (layer, 0, 0)),
            pl.BlockSpec((None, 1, RET_DV), lambda h, i: (layer, 0, h)),
        ],
        out_specs=pl.BlockSpec((rows, RET_DV), lambda h, i: (i, h)),
        out_shape=jax.ShapeDtypeStruct((s, ret_v_w), BF16),
        scratch_shapes=[pltpu.VMEM((RET_DK, RET_DV), F32)],
        compiler_params=_params(("arbitrary", "arbitrary"), est),
        name="retention",
    )(z, z, z, z, cos, sin, decay_in, qd, kd, cd,
      gq.reshape(-1, 1, RET_DK), gk.reshape(-1, 1, RET_DK), gn.reshape(gn.shape[0], 1, ret_v_w))


def _gm_kernel(u_ref, v_ref, lg_ref, lb_ref, ws_ref, bs_ref, o_ref, wm_ref, *, n_chunks, groups):
    @pl.when(pl.program_id(0) == 0)
    def _():
        row = lax.broadcasted_iota(jnp.int32, (CHUNK, CHUNK), 0)
        col = lax.broadcasted_iota(jnp.int32, (CHUNK, CHUNK), 1)
        for g in range(groups):
            wm_ref[g] = jnp.where(col <= row, ws_ref[g], 0.0).astype(BF16)

    for g in range(groups):
        c0 = g * GM_GROUP_DIM
        for c in range(n_chunks):
            r0 = c * CHUNK
            u = _gelu_tanh(u_ref[r0:r0 + CHUNK, c0:c0 + GM_GROUP_DIM].astype(F32))
            v = _gelu_tanh(v_ref[r0:r0 + CHUNK, c0:c0 + GM_GROUP_DIM].astype(F32))
            mu = jnp.mean(v, axis=-1, keepdims=True)
            vc = v - mu
            vn = vc * lax.rsqrt(jnp.mean(vc * vc, axis=-1, keepdims=True) + EPS)
            vn = vn * lg_ref[:, c0:c0 + GM_GROUP_DIM] + lb_ref[:, c0:c0 + GM_GROUP_DIM]
            mixed = jnp.dot(wm_ref[g], vn.astype(BF16), preferred_element_type=F32) + bs_ref[g]
            o_ref[r0:r0 + CHUNK, c0:c0 + GM_GROUP_DIM] = (u * mixed).astype(o_ref.dtype)


def _spatial_gating(z, ln_g, ln_b, w_s, b_s, layer, off_u, gm_w):
    s = z.shape[0]
    groups = gm_w // GM_GROUP_DIM
    rows = _tile("gm_rows", s)
    assert rows % CHUNK == 0 and off_u % gm_w == 0
    b_full = jnp.broadcast_to(b_s[:, :, :, None], b_s.shape + (GM_GROUP_DIM,))
    kern = functools.partial(_gm_kernel, n_chunks=rows // CHUNK, groups=groups)
    est = (6 * _nbytes((rows, gm_w), BF16) + 2 * _nbytes((groups, CHUNK, CHUNK), F32)
           + 2 * _nbytes((groups, CHUNK, GM_GROUP_DIM), F32) + _nbytes((groups, CHUNK, CHUNK), BF16))
    return pl.pallas_call(
        kern,
        grid=(s // rows,),
        in_specs=[
            pl.BlockSpec((rows, gm_w), lambda i: (i, off_u // gm_w)),
            pl.BlockSpec((rows, gm_w), lambda i: (i, off_u // gm_w + 1)),
            pl.BlockSpec((None, 1, gm_w), lambda i: (layer, 0, 0)),
            pl.BlockSpec((None, 1, gm_w), lambda i: (layer, 0, 0)),
            pl.BlockSpec((None, groups, CHUNK, CHUNK), lambda i: (layer, 0, 0, 0)),
            pl.BlockSpec((None, groups, CHUNK, GM_GROUP_DIM), lambda i: (layer, 0, 0, 0)),
        ],
        out_specs=pl.BlockSpec((rows, gm_w), lambda i: (i, 0)),
        out_shape=jax.ShapeDtypeStruct((s, gm_w), BF16),
        scratch_shapes=[pltpu.VMEM((groups, CHUNK, CHUNK), BF16)],
        compiler_params=_params(("arbitrary",), est),
        name="spatial_gating",
    )(z, z, ln_g.reshape(-1, 1, gm_w), ln_b.reshape(-1, 1, gm_w), w_s, b_full)


def _ple_emb_kernel(p_ref, w_ref, g_ref, o_ref, wb_ref):
    @pl.when(pl.program_id(0) == 0)
    def _():
        wb_ref[...] = w_ref[...].astype(BF16)

    e = jnp.dot(p_ref[...].astype(BF16), wb_ref[...], preferred_element_type=F32)
    inv = lax.rsqrt(jnp.mean(e * e, axis=-1, keepdims=True) + EPS)
    o_ref[...] = (e * inv * g_ref[...]).astype(o_ref.dtype)


def _ple_embedding(p, w_proj, post_norm, layer):
    _, _, s, pd = p.shape
    d = w_proj.shape[-1]
    bm = _tile("ple_bm", s)
    est = (2 * _nbytes((bm, pd), F32) + 2 * _nbytes((pd, d), F32) + _nbytes((pd, d), BF16)
           + 2 * _nbytes((bm, d), BF16) + 3 * _nbytes((bm, d), F32))
    return pl.pallas_call(
        _ple_emb_kernel,
        grid=(s // bm,),
        in_specs=[
            pl.BlockSpec((None, None, bm, pd), lambda i: (layer, 0, i, 0)),
            pl.BlockSpec((None, pd, d), lambda i: (layer, 0, 0)),
            pl.BlockSpec((None, 1, d), lambda i: (layer, 0, 0)),
        ],
        out_specs=pl.BlockSpec((bm, d), lambda i: (i, 0)),
        out_shape=jax.ShapeDtypeStruct((s, d), BF16),
        scratch_shapes=[pltpu.VMEM((pd, d), BF16)],
        compiler_params=_params(("arbitrary",), est),
        name="ple_embedding",
    )(p, w_proj, post_norm.reshape(-1, 1, d))


def kernel(x, p, ffn1_norm, ffn1_w_gu, ffn1_w_down, mix_norm, w_in, sb_q_norm, sb_k_norm,
           ret_q_norm, ret_k_norm, ret_gn_g, gm_ln_g, gm_ln_b, gm_w_s, gm_b_s, w_branch, w_out,
           ffn2_norm, ffn2_w_gu, ffn2_w_down, ple_norm, ple_w_gate, ple_w_proj, ple_post_norm):
    batch, s, d = x.shape
    assert batch == 1
    depth = w_in.shape[0]
    n_in = w_in.shape[-1]
    branch_w = w_branch.shape[2]
    sb_w = branch_w
    ret_v_w = branch_w
    ret_qk_w = (ret_v_w // RET_DV) * RET_DK
    gm_w = branch_w
    off_ret = 3 * sb_w
    off_gm = off_ret + 2 * ret_qk_w + 2 * ret_v_w
    off_gates = off_gm + 2 * gm_w
    assert off_gates + N_BRANCH * d == n_in

    bm = _tile("mm_bm", s)
    bn = _tile("mm_bn", d)
    merge_bm = _tile("merge_bm", s)
    assert n_in % bn == 0 and off_gates % bn == 0

    h = x.reshape(s, d)
    for layer in range(depth):
        n = _rmsnorm(h, ffn1_norm, layer)
        h = _swiglu_ffn_half(h, n, ffn1_w_gu, ffn1_w_down, layer)

        n = _rmsnorm(h, mix_norm, layer)
        z = _matmul(xs=[(n, d, 0)], ws=[(w_in, (layer,), 0, 0, 0)], extras=[],
                    epilogue=_ep_identity, n_out=n_in, out_dtype=BF16, bm=bm, bn=bn, name="w_in")
        y_a = _stick_breaking(z, sb_q_norm, sb_k_norm, layer, sb_w)
        y_b = _retention(z, ret_q_norm, ret_k_norm, ret_gn_g, layer, off_ret, ret_qk_w, ret_v_w)
        y_c = _spatial_gating(z, gm_ln_g, gm_ln_b, gm_w_s, gm_b_s, layer, off_gm, gm_w)
        merged = _matmul(
            xs=[(y_a, branch_w, 0), (y_b, branch_w, 0), (y_c, branch_w, 0)],
            ws=[(w_branch, (layer, b), b, 0, 0) for b in range(N_BRANCH)],
            extras=[(z, (off_gates + b * d) // bn) for b in range(N_BRANCH)],
            epilogue=_ep_merge, n_out=d, out_dtype=BF16, bm=merge_bm, bn=bn, name="branch_merge")
        h = _matmul(xs=[(merged, d, 0)], ws=[(w_out, (layer,), 0, 0, 0)], extras=[(h, 0)],
                    epilogue=_ep_residual(1.0), n_out=d, out_dtype=F32, bm=bm, bn=bn, name="w_out")

        n = _rmsnorm(h, ffn2_norm, layer)
        h = _swiglu_ffn_half(h, n, ffn2_w_gu, ffn2_w_down, layer)

        n = _rmsnorm(h, ple_norm, layer)
        emb = _ple_embedding(p, ple_w_proj, ple_post_norm, layer)
        h = _matmul(xs=[(n, d, 0)], ws=[(ple_w_gate, (layer,), 0, 0, 0)],
                    extras=[(h, 0), (emb, 0)], epilogue=_ep_ple, n_out=d, out_dtype=F32,
                    bm=bm, bn=bn, name="ple_gate")
    return h.reshape(batch, s, d)
```

```python
import functools

import jax
import jax.numpy as jnp
from jax import lax
from jax.experimental import pallas as pl
from jax.experimental.pallas import tpu as pltpu

F32 = jnp.float32
BF16 = jnp.bfloat16

HEAD_DIM = 128
RET_DK = 128
RET_DV = 256
GM_GROUP_DIM = 128
CHUNK = 128
N_BRANCH = 3
ROPE_BASE = 10000.0
EPS = 1e-6

VMEM_BYTES_V7X = 64 * 1024 * 1024
VMEM_CAP = VMEM_BYTES_V7X - 6 * 1024 * 1024

TILES = dict(
    mm_bm=1024,
    mm_bn=512,
    gu_bn=256,
    merge_bm=512,
    norm_bm=256,
    sb_bq=512,
    sb_sub=256,
    sb_heads=2,
    ret_rows=512,
    gm_rows=256,
    ple_bm=256,
)


def _tile(name, dim):
    t = min(TILES[name], dim)
    assert dim % t == 0, (name, dim, t)
    return t


def _params(sem, vmem_est):
    limit = int(min(max(vmem_est * 1.15 + (4 << 20), 32 << 20), VMEM_CAP))
    return pltpu.CompilerParams(dimension_semantics=sem, vmem_limit_bytes=limit)


def _nbytes(shape, dtype):
    n = 1
    for s in shape:
        n *= s
    return n * jnp.dtype(dtype).itemsize


def _sigmoid(x):
    return 1.0 / (1.0 + jnp.exp(-x))


def _silu(x):
    return x * _sigmoid(x)


def _gelu_tanh(x):
    c = 0.7978845608028654
    return 0.5 * x * (1.0 + jnp.tanh(c * (x + 0.044715 * (x * x * x))))


def _rmsnorm_kernel(x_ref, g_ref, o_ref):
    x = x_ref[...]
    inv = lax.rsqrt(jnp.mean(x * x, axis=-1, keepdims=True) + EPS)
    o_ref[...] = (x * inv * g_ref[...]).astype(o_ref.dtype)


def _rmsnorm(h, gains, layer):
    s, d = h.shape
    bm = _tile("norm_bm", s)
    est = 2 * (_nbytes((bm, d), F32) + _nbytes((bm, d), BF16)) + 2 * _nbytes((bm, d), F32)
    return pl.pallas_call(
        _rmsnorm_kernel,
        grid=(s // bm,),
        in_specs=[
            pl.BlockSpec((bm, d), lambda i: (i, 0)),
            pl.BlockSpec((None, 1, d), lambda i: (layer, 0, 0)),
        ],
        out_specs=pl.BlockSpec((bm, d), lambda i: (i, 0)),
        out_shape=jax.ShapeDtypeStruct((s, d), BF16),
        compiler_params=_params(("arbitrary",), est),
        name="rmsnorm",
    )(h, gains.reshape(gains.shape[0], 1, d))


def _mm_kernel(*refs, n_x, x_index, n_e, epilogue):
    n_w = len(x_index)
    x_refs = refs[:n_x]
    w_refs = refs[n_x:n_x + n_w]
    e_refs = refs[n_x + n_w:n_x + n_w + n_e]
    o_ref = refs[n_x + n_w + n_e]
    wb_refs = refs[n_x + n_w + n_e + 1:]

    @pl.when(pl.program_id(1) == 0)
    def _():
        for w_ref, wb_ref in zip(w_refs, wb_refs):
            wb_ref[...] = w_ref[...].astype(BF16)

    accs = [
        jnp.dot(x_refs[xi][...], wb_ref[...], preferred_element_type=F32)
        for xi, wb_ref in zip(x_index, wb_refs)
    ]
    o_ref[...] = epilogue(accs, [e[...] for e in e_refs]).astype(o_ref.dtype)


def _matmul(xs, ws, extras, epilogue, n_out, out_dtype, bm, bn, name):
    s = xs[0][0].shape[0]
    assert s % bm == 0 and n_out % bn == 0
    in_specs, args = [], []
    est = 0
    for arr, k, cb in xs:
        in_specs.append(pl.BlockSpec((bm, k), lambda j, i, cb=cb: (i, cb)))
        args.append(arr)
        est += 2 * _nbytes((bm, k), arr.dtype)
    scratch = []
    for arr, lead, xi, rb, co in ws:
        k = xs[xi][1]
        nlead = len(lead)
        in_specs.append(pl.BlockSpec(
            (None,) * nlead + (k, bn),
            lambda j, i, lead=lead, rb=rb, co=co: tuple(lead) + (rb, j + co)))
        args.append(arr)
        scratch.append(pltpu.VMEM((k, bn), BF16))
        est += 2 * _nbytes((k, bn), arr.dtype) + _nbytes((k, bn), BF16)
        est += _nbytes((bm, bn), F32)
    for arr, co in extras:
        in_specs.append(pl.BlockSpec((bm, bn), lambda j, i, co=co: (i, j + co)))
        args.append(arr)
        est += 2 * _nbytes((bm, bn), arr.dtype)
    est += 2 * _nbytes((bm, bn), out_dtype) + 2 * _nbytes((bm, bn), F32)
    kern = functools.partial(
        _mm_kernel, n_x=len(xs), x_index=tuple(w[2] for w in ws), n_e=len(extras),
        epilogue=epilogue)
    return pl.pallas_call(
        kern,
        grid=(n_out // bn, s // bm),
        in_specs=in_specs,
        out_specs=pl.BlockSpec((bm, bn), lambda j, i: (i, j)),
        out_shape=jax.ShapeDtypeStruct((s, n_out), out_dtype),
        scratch_shapes=scratch,
        compiler_params=_params(("arbitrary", "arbitrary"), est),
        name=name,
    )(*args)


def _ep_swiglu(accs, extras):
    g, u = accs
    return _silu(g) * u


def _ep_identity(accs, extras):
    return accs[0]


def _ep_residual(scale):
    def ep(accs, extras):
        return extras[0] + scale * accs[0]
    return ep


def _ep_merge(accs, extras):
    out = _sigmoid(extras[0].astype(F32)) * accs[0]
    for b in range(1, N_BRANCH):
        out = out + _sigmoid(extras[b].astype(F32)) * accs[b]
    return out


def _ep_ple(accs, extras):
    h, emb = extras
    return h + _sigmoid(accs[0]) * emb.astype(F32)


def _swiglu_ffn_half(h, n, w_gu, w_down, layer):
    s, d = n.shape
    d_ff = w_down.shape[1]
    bm = _tile("mm_bm", s)
    bn_gu = _tile("gu_bn", d_ff)
    a = _matmul(
        xs=[(n, d, 0)],
        ws=[(w_gu, (layer,), 0, 0, 0), (w_gu, (layer,), 0, 0, d_ff // bn_gu)],
        extras=[], epilogue=_ep_swiglu, n_out=d_ff, out_dtype=BF16,
        bm=bm, bn=bn_gu, name="ffn_gate_up")
    bn = _tile("mm_bn", d)
    kc = min(d, d_ff)
    assert d_ff % kc == 0
    for kb in range(d_ff // kc):
        h = _matmul(
            xs=[(a, kc, kb)],
            ws=[(w_down, (layer,), 0, kb, 0)],
            extras=[(h, 0)], epilogue=_ep_residual(0.5), n_out=d, out_dtype=F32,
            bm=bm, bn=bn, name="ffn_down")
    return h


LOG2_E = 1.4426950408889634


def _sb_kernel(q_ref, k_ref, v_ref, gq_ref, gk_ref, u_ref, o_ref, kn_ref, *,
               bq, sub, heads_per_step, q_scale, norm_rows):
    qi = pl.program_id(1)
    s_len = k_ref.shape[0]
    n_sub = bq // sub

    @pl.when(qi == 0)
    def _():
        def norm_chunk(c, carry):
            r0 = pl.multiple_of(c * norm_rows, norm_rows)
            for hh in range(heads_per_step):
                cs = slice(hh * HEAD_DIM, (hh + 1) * HEAD_DIM)
                kf = k_ref[pl.ds(r0, norm_rows), cs].astype(F32)
                inv = lax.rsqrt(jnp.mean(kf * kf, axis=-1, keepdims=True) + EPS)
                kn_ref[pl.ds(r0, norm_rows), cs] = (kf * inv * gk_ref[...]).astype(BF16)
            return carry
        lax.fori_loop(0, s_len // norm_rows, norm_chunk, 0)

    qn = []
    for hh in range(heads_per_step):
        qf = q_ref[:, hh * HEAD_DIM:(hh + 1) * HEAD_DIM].astype(F32)
        inv_q = lax.rsqrt(jnp.mean(qf * qf, axis=-1, keepdims=True) + EPS)
        qn.append((qf * inv_q * (gq_ref[...] * q_scale)).astype(BF16))
    neg_tri = u_ref[...]

    row = lax.broadcasted_iota(jnp.int32, (bq, bq), 0)
    col = lax.broadcasted_iota(jnp.int32, (bq, bq), 1)
    causal = col < row

    def head_tile(hh, r0, carry, acc, masked):
        cs = slice(hh * HEAD_DIM, (hh + 1) * HEAD_DIM)
        kj = kn_ref[pl.ds(r0, bq), cs]
        zz = lax.dot_general(qn[hh], kj, (((1,), (1,)), ((), ())), preferred_element_type=F32)
        neg_abs = pltpu.bitcast(pltpu.bitcast(zz, jnp.uint32) | jnp.uint32(0x80000000), F32)
        soft = jnp.log(1.0 + jnp.exp2(neg_abs)) * LOG2_E
        p = jnp.maximum(zz, 0.0) + soft
        if masked:
            p = jnp.where(causal, p, 0.0)
        for sb in reversed(range(n_sub)):
            ls = slice(sb * sub, (sb + 1) * sub)
            ps = p[:, ls]
            cum = jnp.dot(ps.astype(BF16), neg_tri, preferred_element_type=F32)
            w = jnp.exp2(zz[:, ls] + cum + carry)
            if masked:
                w = jnp.where(causal[:, ls], w, 0.0)
            vj = v_ref[pl.ds(r0 + sb * sub, sub), cs]
            acc = acc + jnp.dot(w.astype(BF16), vj, preferred_element_type=F32)
            carry = carry - jnp.sum(ps, axis=-1, keepdims=True)
        return carry, acc

    def tile(j, state, masked):
        r0 = pl.multiple_of(j * bq, bq)
        return tuple(head_tile(hh, r0, c, a, masked) for hh, (c, a) in enumerate(state))

    state = tuple((jnp.zeros((bq, 1), F32), jnp.zeros((bq, HEAD_DIM), F32))
                  for _ in range(heads_per_step))
    state = tile(qi, state, True)
    state = lax.fori_loop(0, qi, lambda it, st: tile(qi - 1 - it, st, False), state)
    for hh, (_, acc) in enumerate(state):
        o_ref[:, hh * HEAD_DIM:(hh + 1) * HEAD_DIM] = acc.astype(o_ref.dtype)


def _stick_breaking(z, gq, gk, layer, sb_w):
    s = z.shape[0]
    heads = sb_w // HEAD_DIM
    hps = TILES["sb_heads"] if heads % TILES["sb_heads"] == 0 else 1
    groups = heads // hps
    gw = hps * HEAD_DIM
    bq = _tile("sb_bq", s)
    sub = _tile("sb_sub", bq)
    neg_tri = -(jnp.arange(sub)[:, None] >= jnp.arange(sub)[None, :]).astype(BF16)
    kern = functools.partial(_sb_kernel, bq=bq, sub=sub, heads_per_step=hps,
                             q_scale=LOG2_E * HEAD_DIM ** -0.5, norm_rows=min(512, s))
    est = (5 * _nbytes((s, gw), BF16) + hps * 8 * _nbytes((bq, bq), F32))
    return pl.pallas_call(
        kern,
        grid=(groups, s // bq),
        in_specs=[
            pl.BlockSpec((bq, gw), lambda h, i: (i, h)),
            pl.BlockSpec((s, gw), lambda h, i: (0, groups + h)),
            pl.BlockSpec((s, gw), lambda h, i: (0, 2 * groups + h)),
            pl.BlockSpec((None, 1, HEAD_DIM), lambda h, i: (layer, 0, 0)),
            pl.BlockSpec((None, 1, HEAD_DIM), lambda h, i: (layer, 0, 0)),
            pl.BlockSpec((sub, sub), lambda h, i: (0, 0)),
        ],
        out_specs=pl.BlockSpec((bq, gw), lambda h, i: (i, h)),
        out_shape=jax.ShapeDtypeStruct((s, sb_w), BF16),
        scratch_shapes=[pltpu.VMEM((s, gw), BF16)],
        compiler_params=_params(("arbitrary", "arbitrary"), est),
        name="stick_breaking",
    )(z, z, z, gq.reshape(-1, 1, HEAD_DIM), gk.reshape(-1, 1, HEAD_DIM), neg_tri)


def _ret_kernel(q_ref, k_ref, v_ref, g_ref, cos_ref, sin_ref, din_ref, qd_ref, kd_ref, cd_ref,
                gq_ref, gk_ref, gn_ref, o_ref, state_ref, *, n_chunks, k_scale):
    @pl.when(pl.program_id(1) == 0)
    def _():
        state_ref[...] = jnp.zeros_like(state_ref)

    def normed_rope(x_ref, gain_ref, r0):
        x = x_ref[r0:r0 + CHUNK, :].astype(F32)
        x = x * lax.rsqrt(jnp.mean(x * x, axis=-1, keepdims=True) + EPS) * gain_ref[...]
        swapped = pltpu.roll(x, RET_DK // 2, 1)
        return x * cos_ref[r0:r0 + CHUNK, :] + swapped * sin_ref[r0:r0 + CHUNK, :]

    for c in range(n_chunks):
        r0 = c * CHUNK
        q = normed_rope(q_ref, gq_ref, r0)
        k = normed_rope(k_ref, gk_ref, r0) * k_scale
        v = v_ref[r0:r0 + CHUNK, :]
        qb = q.astype(BF16)
        inner = lax.dot_general(qb, k.astype(BF16), (((1,), (1,)), ((), ())),
                                preferred_element_type=F32) * din_ref[...]
        state = state_ref[...]
        o = jnp.dot(inner.astype(BF16), v, preferred_element_type=F32)
        o = o + jnp.dot(qb, state.astype(BF16), preferred_element_type=F32) * qd_ref[...]
        kd = (k * kd_ref[...]).astype(BF16)
        state_ref[...] = state * cd_ref[...] + lax.dot_general(
            kd, v, (((0,), (0,)), ((), ())), preferred_element_type=F32)
        mu = jnp.mean(o, axis=-1, keepdims=True)
        oc = o - mu
        y = oc * lax.rsqrt(jnp.mean(oc * oc, axis=-1, keepdims=True) + EPS) * gn_ref[...]
        gate = g_ref[r0:r0 + CHUNK, :].astype(F32)
        o_ref[r0:r0 + CHUNK, :] = (_silu(gate) * y).astype(o_ref.dtype)


def _retention_tables(heads, s):
    hh = jnp.arange(heads, dtype=F32)
    log_gamma = jnp.log(1.0 - 2.0 ** (-5.0 - hh))
    idx = jnp.arange(CHUNK, dtype=F32)
    diff = idx[:, None] - idx[None, :]
    decay_in = jnp.where(diff >= 0, jnp.exp(log_gamma[:, None, None] * jnp.maximum(diff, 0.0)), 0.0)
    q_decay = jnp.exp(log_gamma[:, None] * (idx + 1.0))
    k_decay = jnp.exp(log_gamma[:, None] * (CHUNK - 1.0 - idx))
    chunk_decay = jnp.exp(log_gamma * CHUNK)
    qd = jnp.broadcast_to(q_decay[:, :, None], (heads, CHUNK, RET_DV))
    kd = jnp.broadcast_to(k_decay[:, :, None], (heads, CHUNK, RET_DK))
    cd = jnp.broadcast_to(chunk_decay[:, None, None], (heads, 1, RET_DV))
    half = RET_DK // 2
    freq = ROPE_BASE ** (-jnp.arange(half, dtype=F32) / half)
    ang = jnp.arange(s, dtype=F32)[:, None] * freq[None, :]
    cos = jnp.concatenate([jnp.cos(ang), jnp.cos(ang)], axis=-1)
    sin = jnp.concatenate([-jnp.sin(ang), jnp.sin(ang)], axis=-1)
    return decay_in, qd, kd, cd, cos, sin


def _retention(z, gq, gk, gn, layer, off_q, ret_qk_w, ret_v_w):
    s = z.shape[0]
    heads = ret_v_w // RET_DV
    rows = _tile("ret_rows", s)
    assert rows % CHUNK == 0
    decay_in, qd, kd, cd, cos, sin = _retention_tables(heads, s)
    off_k = off_q + ret_qk_w
    off_v = off_k + ret_qk_w
    off_g = off_v + ret_v_w
    assert off_q % RET_DK == 0 and off_v % RET_DV == 0 and off_g % RET_DV == 0
    kern = functools.partial(_ret_kernel, n_chunks=rows // CHUNK, k_scale=RET_DK ** -0.5)
    est = 2 * (2 * _nbytes((rows, RET_DK), BF16) + 3 * _nbytes((rows, RET_DV), BF16)
               + 2 * _nbytes((rows, RET_DK), F32) + 4 * _nbytes((CHUNK, RET_DV), F32))
    return pl.pallas_call(
        kern,
        grid=(heads, s // rows),
        in_specs=[
            pl.BlockSpec((rows, RET_DK), lambda h, i: (i, off_q // RET_DK + h)),
            pl.BlockSpec((rows, RET_DK), lambda h, i: (i, off_k // RET_DK + h)),
            pl.BlockSpec((rows, RET_DV), lambda h, i: (i, off_v // RET_DV + h)),
            pl.BlockSpec((rows, RET_DV), lambda h, i: (i, off_g // RET_DV + h)),
            pl.BlockSpec((rows, RET_DK), lambda h, i: (i, 0)),
            pl.BlockSpec((rows, RET_DK), lambda h, i: (i, 0)),
            pl.BlockSpec((None, CHUNK, CHUNK), lambda h, i: (h, 0, 0)),
            pl.BlockSpec((None, CHUNK, RET_DV), lambda h, i: (h, 0, 0)),
            pl.BlockSpec((None, CHUNK, RET_DK), lambda h, i: (h, 0, 0)),
            pl.BlockSpec((None, 1, RET_DV), lambda h, i: (h, 0, 0)),
            pl.BlockSpec((None, 1, RET_DK), lambda h, i: (layer, 0, 0)),
            pl.BlockSpec((None, 1, RET_DK), lambda h, i: (layer, 0, 0)),
            pl.BlockSpec((None, 1, RET_DV), lambda h, i: (layer, 0, h)),
        ],
        out_specs=pl.BlockSpec((rows, RET_DV), lambda h, i: (i, h)),
        out_shape=jax.ShapeDtypeStruct((s, ret_v_w), BF16),
        scratch_shapes=[pltpu.VMEM((RET_DK, RET_DV), F32)],
        compiler_params=_params(("arbitrary", "arbitrary"), est),
        name="retention",
    )(z, z, z, z, cos, sin, decay_in, qd, kd, cd,
      gq.reshape(-1, 1, RET_DK), gk.reshape(-1, 1, RET_DK), gn.reshape(gn.shape[0], 1, ret_v_w))


def _gm_kernel(u_ref, v_ref, lg_ref, lb_ref, ws_ref, bs_ref, o_ref, wm_ref, *, n_chunks, groups):
    @pl.when(pl.program_id(0) == 0)
    def _():
        row = lax.broadcasted_iota(jnp.int32, (CHUNK, CHUNK), 0)
        col = lax.broadcasted_iota(jnp.int32, (CHUNK, CHUNK), 1)
        for g in range(groups):
            wm_ref[g] = jnp.where(col <= row, ws_ref[g], 0.0).astype(BF16)

    for g in range(groups):
        c0 = g * GM_GROUP_DIM
        for c in range(n_chunks):
            r0 = c * CHUNK
            u = _gelu_tanh(u_ref[r0:r0 + CHUNK, c0:c0 + GM_GROUP_DIM].astype(F32))
            v = _gelu_tanh(v_ref[r0:r0 + CHUNK, c0:c0 + GM_GROUP_DIM].astype(F32))
            mu = jnp.mean(v, axis=-1, keepdims=True)
            vc = v - mu
            vn = vc * lax.rsqrt(jnp.mean(vc * vc, axis=-1, keepdims=True) + EPS)
            vn = vn * lg_ref[:, c0:c0 + GM_GROUP_DIM] + lb_ref[:, c0:c0 + GM_GROUP_DIM]
            mixed = jnp.dot(wm_ref[g], vn.astype(BF16), preferred_element_type=F32) + bs_ref[g]
            o_ref[r0:r0 + CHUNK, c0:c0 + GM_GROUP_DIM] = (u * mixed).astype(o_ref.dtype)


def _spatial_gating(z, ln_g, ln_b, w_s, b_s, layer, off_u, gm_w):
    s = z.shape[0]
    groups = gm_w // GM_GROUP_DIM
    rows = _tile("gm_rows", s)
    assert rows % CHUNK == 0 and off_u % gm_w == 0
    b_full = jnp.broadcast_to(b_s[:, :, :, None], b_s.shape + (GM_GROUP_DIM,))
    kern = functools.partial(_gm_kernel, n_chunks=rows // CHUNK, groups=groups)
    est = (6 * _nbytes((rows, gm_w), BF16) + 2 * _nbytes((groups, CHUNK, CHUNK), F32)
           + 2 * _nbytes((groups, CHUNK, GM_GROUP_DIM), F32) + _nbytes((groups, CHUNK, CHUNK), BF16))
    return pl.pallas_call(
        kern,
        grid=(s // rows,),
        in_specs=[
            pl.BlockSpec((rows, gm_w), lambda i: (i, off_u // gm_w)),
            pl.BlockSpec((rows, gm_w), lambda i: (i, off_u // gm_w + 1)),
            pl.BlockSpec((None, 1, gm_w), lambda i: (layer, 0, 0)),
            pl.BlockSpec((None, 1, gm_w), lambda i: (layer, 0, 0)),
            pl.BlockSpec((None, groups, CHUNK, CHUNK), lambda i: (layer, 0, 0, 0)),
            pl.BlockSpec((None, groups, CHUNK, GM_GROUP_DIM), lambda i: (layer, 0, 0, 0)),
        ],
        out_specs=pl.BlockSpec((rows, gm_w), lambda i: (i, 0)),
        out_shape=jax.ShapeDtypeStruct((s, gm_w), BF16),
        scratch_shapes=[pltpu.VMEM((groups, CHUNK, CHUNK), BF16)],
        compiler_params=_params(("arbitrary",), est),
        name="spatial_gating",
    )(z, z, ln_g.reshape(-1, 1, gm_w), ln_b.reshape(-1, 1, gm_w), w_s, b_full)


def _ple_emb_kernel(p_ref, w_ref, g_ref, o_ref, wb_ref):
    @pl.when(pl.program_id(0) == 0)
    def _():
        wb_ref[...] = w_ref[...].astype(BF16)

    e = jnp.dot(p_ref[...].astype(BF16), wb_ref[...], preferred_element_type=F32)
    inv = lax.rsqrt(jnp.mean(e * e, axis=-1, keepdims=True) + EPS)
    o_ref[...] = (e * inv * g_ref[...]).astype(o_ref.dtype)


def _ple_embedding(p, w_proj, post_norm, layer):
    _, _, s, pd = p.shape
    d = w_proj.shape[-1]
    bm = _tile("ple_bm", s)
    est = (2 * _nbytes((bm, pd), F32) + 2 * _nbytes((pd, d), F32) + _nbytes((pd, d), BF16)
           + 2 * _nbytes((bm, d), BF16) + 3 * _nbytes((bm, d), F32))
    return pl.pallas_call(
        _ple_emb_kernel,
        grid=(s // bm,),
        in_specs=[
            pl.BlockSpec((None, None, bm, pd), lambda i: (layer, 0, i, 0)),
            pl.BlockSpec((None, pd, d), lambda i: (layer, 0, 0)),
            pl.BlockSpec((None, 1, d), lambda i: (layer, 0, 0)),
        ],
        out_specs=pl.BlockSpec((bm, d), lambda i: (i, 0)),
        out_shape=jax.ShapeDtypeStruct((s, d), BF16),
        scratch_shapes=[pltpu.VMEM((pd, d), BF16)],
        compiler_params=_params(("arbitrary",), est),
        name="ple_embedding",
    )(p, w_proj, post_norm.reshape(-1, 1, d))


def kernel(x, p, ffn1_norm, ffn1_w_gu, ffn1_w_down, mix_norm, w_in, sb_q_norm, sb_k_norm,
           ret_q_norm, ret_k_norm, ret_gn_g, gm_ln_g, gm_ln_b, gm_w_s, gm_b_s, w_branch, w_out,
           ffn2_norm, ffn2_w_gu, ffn2_w_down, ple_norm, ple_w_gate, ple_w_proj, ple_post_norm):
    batch, s, d = x.shape
    assert batch == 1
    depth = w_in.shape[0]
    n_in = w_in.shape[-1]
    branch_w = w_branch.shape[2]
    sb_w = branch_w
    ret_v_w = branch_w
    ret_qk_w = (ret_v_w // RET_DV) * RET_DK
    gm_w = branch_w
    off_ret = 3 * sb_w
    off_gm = off_ret + 2 * ret_qk_w + 2 * ret_v_w
    off_gates = off_gm + 2 * gm_w
    assert off_gates + N_BRANCH * d == n_in

    bm = _tile("mm_bm", s)
    bn = _tile("mm_bn", d)
    merge_bm = _tile("merge_bm", s)
    assert n_in % bn == 0 and off_gates % bn == 0

    h = x.reshape(s, d)
    for layer in range(depth):
        n = _rmsnorm(h, ffn1_norm, layer)
        h = _swiglu_ffn_half(h, n, ffn1_w_gu, ffn1_w_down, layer)

        n = _rmsnorm(h, mix_norm, layer)
        z = _matmul(xs=[(n, d, 0)], ws=[(w_in, (layer,), 0, 0, 0)], extras=[],
                    epilogue=_ep_identity, n_out=n_in, out_dtype=BF16, bm=bm, bn=bn, name="w_in")
        y_a = _stick_breaking(z, sb_q_norm, sb_k_norm, layer, sb_w)
        y_b = _retention(z, ret_q_norm, ret_k_norm, ret_gn_g, layer, off_ret, ret_qk_w, ret_v_w)
        y_c = _spatial_gating(z, gm_ln_g, gm_ln_b, gm_w_s, gm_b_s, layer, off_gm, gm_w)
        merged = _matmul(
            xs=[(y_a, branch_w, 0), (y_b, branch_w, 0), (y_c, branch_w, 0)],
            ws=[(w_branch, (layer, b), b, 0, 0) for b in range(N_BRANCH)],
            extras=[(z, (off_gates + b * d) // bn) for b in range(N_BRANCH)],
            epilogue=_ep_merge, n_out=d, out_dtype=BF16, bm=merge_bm, bn=bn, name="branch_merge")
        h = _matmul(xs=[(merged, d, 0)], ws=[(w_out, (layer,), 0, 0, 0)], extras=[(h, 0)],
                    epilogue=_ep_residual(1.0), n_out=d, out_dtype=F32, bm=bm, bn=bn, name="w_out")

        n = _rmsnorm(h, ffn2_norm, layer)
        h = _swiglu_ffn_half(h, n, ffn2_w_gu, ffn2_w_down, layer)

        n = _rmsnorm(h, ple_norm, layer)
        emb = _ple_embedding(p, ple_w_proj, ple_post_norm, layer)
        h = _matmul(xs=[(n, d, 0)], ws=[(ple_w_gate, (layer,), 0, 0, 0)],
                    extras=[(h, 0), (emb, 0)], epilogue=_ep_ple, n_out=d, out_dtype=F32,
                    bm=bm, bn=bn, name="ple_gate")
    return h.reshape(batch, s, d)
```
